```python
import math
import jax, jax.numpy as jnp
from jax import lax
import numpy as np

D_MODEL = 1024
BATCH = 16
SEQ = 2048
DEPTH = 4

ROPE_THETA = 10000.0
LN_EPS = 1e-5
Q_BLOCK = 128
M_HEADS = 4
M_DQK = 64
M_DV = 64
M_CHUNK = 64
CONV_WIDTH = 4
DF_HEADS = 4
DF_DQK = 64
DF_DV = 2 * DF_DQK
SA_HEADS = 4
SA_DH = 64
IDX_HEADS = 8
IDX_DIM = 64
INDEX_TOPK_MAX = 256
N_EXPERTS = 32
TOP_K = 4
D_FF = D_MODEL
SWIGLU_LIMIT = 7.0
SWIGLU_ALPHA = 1.702
DEEPNORM_ALPHA = (2 * DEPTH) ** 0.25
DEEPNORM_BETA = (8 * DEPTH) ** -0.25
N_MOD = 6

M_QK_W = M_HEADS * M_DQK
M_V_W = M_HEADS * M_DV
DF_QK_W = 2 * DF_HEADS * DF_DQK
DF_V_W = DF_HEADS * DF_DV
SA_W = SA_HEADS * SA_DH
IDX_Q_W = IDX_HEADS * IDX_DIM
IN_SPLITS = (M_QK_W, M_QK_W, M_V_W, M_HEADS, M_HEADS, M_V_W,
             DF_QK_W, DF_QK_W, DF_V_W,
             SA_W, SA_W, SA_W,
             IDX_Q_W, IDX_DIM, IDX_HEADS)
N_IN_COLS = sum(IN_SPLITS)
D_MIX = M_V_W + DF_V_W + SA_W

kernel_name = "hymba_style_mlstm_diffattn_dsa_moe_trunk"


def _split_columns(h):
    offs, acc = [], 0
    for w in IN_SPLITS[:-1]:
        acc += w
        offs.append(acc)
    return jnp.split(h, offs, axis=-1)


def layer_norm(x, g, b):
    xf = x.astype(jnp.float32)
    mu = xf.mean(-1, keepdims=True)
    var = jnp.square(xf - mu).mean(-1, keepdims=True)
    return ((xf - mu) * lax.rsqrt(var + LN_EPS)).astype(x.dtype) * g + b


def head_norm(h, g, center):
    hf = h.astype(jnp.float32)
    if center:
        hf = hf - hf.mean(-1, keepdims=True)
    hf = hf * lax.rsqrt(jnp.mean(jnp.square(hf), -1, keepdims=True) + LN_EPS)
    return (hf * g).astype(g.dtype)


def rope_tables(positions, dim):
    inv = ROPE_THETA ** (-jnp.arange(0, dim, 2, dtype=jnp.float32) / dim)
    ang = positions.astype(jnp.float32)[..., None] * inv
    return jnp.cos(ang)[:, :, None, :], jnp.sin(ang)[:, :, None, :]


def apply_rope(x, cos, sin):
    x1, x2 = jnp.split(x, 2, axis=-1)
    c = cos.astype(x.dtype)
    s = sin.astype(x.dtype)
    return jnp.concatenate([x1 * c - x2 * s, x2 * c + x1 * s], axis=-1)


def causal_conv(x, w, b):
    y = lax.conv_general_dilated(x, w[:, None, :], window_strides=(1,),
                                 padding=[(CONV_WIDTH - 1, 0)],
                                 dimension_numbers=("NWC", "WIO", "NWC"),
                                 feature_group_count=x.shape[-1])
    return y + b


def mlstm_chunkwise(q, k, v, i_pre, f_pre):
    B, S, H, dk = q.shape
    dv = v.shape[-1]
    nc = S // M_CHUNK
    f32 = jnp.float32

    def to_chunks(a):
        a = a.astype(f32).reshape(B, nc, M_CHUNK, H, *a.shape[3:])
        return jnp.moveaxis(a, (1, 3), (0, 2))

    qc = to_chunks(q) * dk ** -0.5
    kc = to_chunks(k)
    vc = to_chunks(v)
    ic = to_chunks(i_pre)
    lfc = to_chunks(jax.nn.log_sigmoid(f_pre.astype(f32)))
    causal = jnp.tril(jnp.ones((M_CHUNK, M_CHUNK), dtype=bool))

    def step(carry, xs):
        C, n, m = carry
        qi, ki, vi, ii, lf = xs
        bcum = jnp.cumsum(lf, axis=-1)
        logd = bcum[..., :, None] - bcum[..., None, :] + ii[..., None, :]
        logd = jnp.where(causal, logd, -jnp.inf)
        log_inter = bcum + m[..., None]
        m_t = jnp.maximum(log_inter, logd.max(-1))
        dmat = jnp.exp(logd - m_t[..., None])
        inter = jnp.exp(log_inter - m_t)
        s = jnp.einsum("bhtd,bhsd->bhts", qi, ki) * dmat
        num = jnp.einsum("bhts,bhsv->bhtv", s, vi) + inter[..., None] * jnp.einsum("bhtd,bhdv->bhtv", qi, C)
        den = s.sum(-1) + inter * jnp.einsum("bhtd,bhd->bht", qi, n)
        h = num / jnp.maximum(jnp.abs(den), jnp.exp(-m_t))[..., None]
        b_last = bcum[..., -1]
        log_w = b_last[..., None] - bcum + ii
        m_new = jnp.maximum(b_last + m, log_w.max(-1))
        w = jnp.exp(log_w - m_new[..., None])
        decay = jnp.exp(b_last + m - m_new)
        C_new = decay[..., None, None] * C + jnp.einsum("bhs,bhsd,bhsv->bhdv", w, ki, vi)
        n_new = decay[..., None] * n + jnp.einsum("bhs,bhsd->bhd", w, ki)
        return (C_new, n_new, m_new), h

    init = (jnp.zeros((B, H, dk, dv), f32), jnp.zeros((B, H, dk), f32), jnp.zeros((B, H), f32))
    _, hc = lax.scan(step, init, (qc, kc, vc, ic, lfc))
    return jnp.moveaxis(hc, (0, 2), (1, 3)).reshape(B, S, H, dv)


def diff_attention(q, k, v, lam):
    B, S, H2, d = q.shape
    H = H2 // 2
    dv = v.shape[-1]
    nb = S // Q_BLOCK
    qb = q.reshape(B, nb, Q_BLOCK, H2, d).swapaxes(0, 1)
    key_pos = jnp.arange(S)

    def block(args):
        qi, blk = args
        s = jnp.einsum("bqgd,bkgd->bgqk", qi, k).astype(jnp.float32) * d ** -0.5
        qpos = blk * Q_BLOCK + jnp.arange(Q_BLOCK)
        s = jnp.where(key_pos[None, :] <= qpos[:, None], s, -jnp.inf)
        p = jax.nn.softmax(s, axis=-1).reshape(B, H, 2, Q_BLOCK, S)
        a = p[:, :, 0] - lam * p[:, :, 1]
        return jnp.einsum("bhqk,bkhv->bqhv", a.astype(v.dtype), v)

    out = lax.map(block, (qb, jnp.arange(nb)))
    return out.swapaxes(0, 1).reshape(B, S, H, dv)


def dsa_attention(q, k, v, q_idx, k_idx, w_idx):
    B, S, H, d = q.shape
    topk = min(INDEX_TOPK_MAX, S // 4)
    nb = S // Q_BLOCK
    key_pos = jnp.arange(S)

    def to_blocks(a):
        return a.reshape(B, nb, Q_BLOCK, *a.shape[2:]).swapaxes(0, 1)

    gather = jax.vmap(lambda a, i: a[i])

    def block(args):
        qi, qxi, wi, blk = args
        qpos = blk * Q_BLOCK + jnp.arange(Q_BLOCK)
        logits = jnp.einsum("bqjd,bkd->bqjk", qxi, k_idx).astype(jnp.float32) * IDX_DIM ** -0.5
        score = jnp.einsum("bqj,bqjk->bqk", wi.astype(jnp.float32) * IDX_HEADS ** -0.5, jax.nn.relu(logits))
        score = jnp.where((key_pos[None, :] <= qpos[:, None])[None], score, -jnp.inf)
        _, idx = lax.top_k(score, topk)
        valid = idx <= qpos[None, :, None]
        flat = idx.reshape(B, Q_BLOCK * topk)
        k_sel = gather(k, flat).reshape(B, Q_BLOCK, topk, H, d)
        v_sel = gather(v, flat).reshape(B, Q_BLOCK, topk, H, d)
        s = jnp.einsum("bqhd,bqkhd->bqhk", qi, k_sel).astype(jnp.float32) * d ** -0.5
        s = jnp.where(valid[:, :, None, :], s, -jnp.inf)
        p = jax.nn.softmax(s, axis=-1)
        return jnp.einsum("bqhk,bqkhd->bqhd", p.astype(v.dtype), v_sel)

    out = lax.map(block, (to_blocks(q), to_blocks(q_idx), to_blocks(w_idx), jnp.arange(nb)))
    return out.swapaxes(0, 1).reshape(B, S, H, d)


def hybrid_mixer(u, cos, sin, layer, w_in, m_gate_bias, m_conv_w, m_conv_b, m_norm_g,
                 df_lambda, df_norm_g, idx_norm_g, idx_norm_b, w_out):
    B, S, _ = u.shape
    h = jnp.einsum("bsd,dc->bsc", u, w_in)
    (mq, mk, mv, mi, mf, mo, dq, dk, dv, sq, sk, sv, xq, xk, xw) = _split_columns(h)

    qk = jax.nn.silu(causal_conv(jnp.concatenate([mq, mk], axis=-1), m_conv_w, m_conv_b))
    mq, mk = jnp.split(qk, 2, axis=-1)
    i_pre = mi + m_gate_bias[:M_HEADS]
    f_pre = mf + m_gate_bias[M_HEADS:]
    hm = mlstm_chunkwise(mq.reshape(B, S, M_HEADS, M_DQK), mk.reshape(B, S, M_HEADS, M_DQK),
                         mv.reshape(B, S, M_HEADS, M_DV), i_pre, f_pre)
    hm = head_norm(hm, m_norm_g.reshape(M_HEADS, M_DV), True).reshape(B, S, M_V_W)
    hm = jax.nn.sigmoid(mo) * hm

    lam_init = 0.8 - 0.6 * math.exp(-0.3 * layer)
    lf = df_lambda.astype(jnp.float32)
    lam = jnp.exp(jnp.sum(lf[0] * lf[1])) - jnp.exp(jnp.sum(lf[2] * lf[3])) + lam_init
    dq = apply_rope(dq.reshape(B, S, 2 * DF_HEADS, DF_DQK), cos, sin)
    dk = apply_rope(dk.reshape(B, S, 2 * DF_HEADS, DF_DQK), cos, sin)
    hd = diff_attention(dq, dk, dv.reshape(B, S, DF_HEADS, DF_DV), lam)
    hd = head_norm(hd, df_norm_g.reshape(DF_HEADS, DF_DV), False) * (1.0 - lam_init)
    hd = hd.reshape(B, S, DF_V_W)

    sq = apply_rope(sq.reshape(B, S, SA_HEADS, SA_DH), cos, sin)
    sk = apply_rope(sk.reshape(B, S, SA_HEADS, SA_DH), cos, sin)
    xq = apply_rope(xq.reshape(B, S, IDX_HEADS, IDX_DIM), cos, sin)
    xk = apply_rope(layer_norm(xk, idx_norm_g, idx_norm_b)[:, :, None, :], cos, sin)[:, :, 0, :]
    hs = dsa_attention(sq, sk, sv.reshape(B, S, SA_HEADS, SA_DH), xq, xk, xw)
    hs = hs.reshape(B, S, SA_W)

    return jnp.einsum("bsm,md->bsd", jnp.concatenate([hm, hd, hs], axis=-1), w_out)


def expert_ffn(u, router_w, router_b, w1, b1, w2, b2):
    B, S, D = u.shape
    t = u.reshape(B * S, D)
    logits = (jnp.einsum("td,de->te", t, router_w) + router_b).astype(jnp.float32)
    top_val, top_idx = lax.top_k(logits, TOP_K)
    p = jax.nn.softmax(top_val, axis=-1)
    combine = jnp.einsum("tk,tke->te", p, jax.nn.one_hot(top_idx, N_EXPERTS, dtype=jnp.float32))

    def expert(acc, xs):
        w1e, b1e, w2e, b2e, ce = xs
        hh = t @ w1e + b1e
        gate, up = jnp.split(hh, 2, axis=-1)
        gate = jnp.minimum(gate, SWIGLU_LIMIT)
        up = jnp.clip(up, -SWIGLU_LIMIT, SWIGLU_LIMIT)
        y = ((up + 1.0) * (gate * jax.nn.sigmoid(SWIGLU_ALPHA * gate))) @ w2e + b2e
        return acc + ce[:, None].astype(y.dtype) * y, None

    out, _ = lax.scan(expert, jnp.zeros_like(t), (w1, b1, w2, b2, combine.T))
    return out.reshape(B, S, D)


def setup_inputs(seed: int = 0) -> dict:
    key = jax.random.key(seed)
    ks = jax.random.split(key, 27)
    f32 = jnp.float32

    def nrm(k, shape, s):
        return jax.random.normal(k, shape, f32) * s

    x = nrm(ks[0], (BATCH, SEQ, D_MODEL), 1.0)
    c = nrm(ks[1], (BATCH, D_MODEL), 1.0)
    offset = jax.random.randint(ks[2], (BATCH, 1), 0, 64, dtype=jnp.int32)
    positions = offset + jnp.arange(SEQ, dtype=jnp.int32)[None, :]
    w_in = nrm(ks[3], (DEPTH, D_MODEL, N_IN_COLS), D_MODEL ** -0.5)
    m_gate_bias = jnp.concatenate([nrm(ks[4], (DEPTH, M_HEADS), 0.1),
                                   3.0 + nrm(ks[5], (DEPTH, M_HEADS), 0.5)], axis=-1)
    m_conv_w = nrm(ks[6], (DEPTH, CONV_WIDTH, 2 * M_QK_W), CONV_WIDTH ** -0.5)
    m_conv_b = nrm(ks[7], (DEPTH, 2 * M_QK_W), 0.02)
    m_norm_g = 1.0 + nrm(ks[8], (DEPTH, M_V_W), 0.02)
    df_lambda = nrm(ks[9], (DEPTH, 4, DF_DQK), 0.1)
    df_norm_g = 1.0 + nrm(ks[10], (DEPTH, DF_V_W), 0.02)
    idx_norm_g = 1.0 + nrm(ks[11], (DEPTH, IDX_DIM), 0.02)
    idx_norm_b = nrm(ks[12], (DEPTH, IDX_DIM), 0.02)
    w_out = nrm(ks[13], (DEPTH, D_MIX, D_MODEL), D_MIX ** -0.5 * DEEPNORM_BETA)
    ada_w = nrm(ks[14], (DEPTH, D_MODEL, N_MOD * D_MODEL), 0.1 * D_MODEL ** -0.5)
    ada_b = nrm(ks[15], (DEPTH, N_MOD * D_MODEL), 0.02)
    ln1_g = 1.0 + nrm(ks[16], (DEPTH, D_MODEL), 0.02)
    ln1_b = nrm(ks[17], (DEPTH, D_MODEL), 0.02)
    ln2_g = 1.0 + nrm(ks[18], (DEPTH, D_MODEL), 0.02)
    ln2_b = nrm(ks[19], (DEPTH, D_MODEL), 0.02)
    router_w = nrm(ks[20], (DEPTH, D_MODEL, N_EXPERTS), D_MODEL ** -0.5)
    router_b = nrm(ks[21], (DEPTH, N_EXPERTS), 0.01)
    w1 = nrm(ks[22], (DEPTH, N_EXPERTS, D_MODEL, 2 * D_FF), D_MODEL ** -0.5)
    b1 = nrm(ks[23], (DEPTH, N_EXPERTS, 2 * D_FF), 0.02)
    w2 = nrm(ks[24], (DEPTH, N_EXPERTS, D_FF, D_MODEL), D_FF ** -0.5 * DEEPNORM_BETA)
    b2 = nrm(ks[25], (DEPTH, N_EXPERTS, D_MODEL), 0.02)
    return {"x": x, "c": c, "positions": positions, "w_in": w_in, "m_gate_bias": m_gate_bias,
            "m_conv_w": m_conv_w, "m_conv_b": m_conv_b, "m_norm_g": m_norm_g,
            "df_lambda": df_lambda, "df_norm_g": df_norm_g, "idx_norm_g": idx_norm_g,
            "idx_norm_b": idx_norm_b, "w_out": w_out, "ada_w": ada_w, "ada_b": ada_b,
            "ln1_g": ln1_g, "ln1_b": ln1_b, "ln2_g": ln2_g, "ln2_b": ln2_b,
            "router_w": router_w, "router_b": router_b, "w1": w1, "b1": b1, "w2": w2, "b2": b2}


def reference(x, c, positions, w_in, m_gate_bias, m_conv_w, m_conv_b, m_norm_g,
              df_lambda, df_norm_g, idx_norm_g, idx_norm_b, w_out, ada_w, ada_b,
              ln1_g, ln1_b, ln2_g, ln2_b, router_w, router_b, w1, b1, w2, b2):
    cos, sin = rope_tables(positions, SA_DH)
    c_act = jax.nn.silu(c)
    for l in range(DEPTH):
        mod = jnp.einsum("bd,dm->bm", c_act, ada_w[l]) + ada_b[l]
        sh1, sc1, g1, sh2, sc2, g2 = [m[:, None, :] for m in jnp.split(mod, N_MOD, axis=-1)]
        u = x * (1.0 + sc1) + sh1
        y = hybrid_mixer(u, cos, sin, l, w_in[l], m_gate_bias[l], m_conv_w[l], m_conv_b[l],
                         m_norm_g[l], df_lambda[l], df_norm_g[l], idx_norm_g[l], idx_norm_b[l], w_out[l])
        x = layer_norm(DEEPNORM_ALPHA * x + (1.0 + g1) * y, ln1_g[l], ln1_b[l])
        u = x * (1.0 + sc2) + sh2
        y = expert_ffn(u, router_w[l], router_b[l], w1[l], b1[l], w2[l], b2[l])
        x = layer_norm(DEEPNORM_ALPHA * x + (1.0 + g2) * y, ln2_g[l], ln2_b[l])
    return x
```

```python
import functools
import math

import jax
import jax.numpy as jnp
from jax import lax
from jax.experimental import pallas as pl
from jax.experimental.pallas import tpu as pltpu

F32, BF16, I32 = jnp.float32, jnp.bfloat16, jnp.int32

ROPE_THETA = 10000.0
LN_EPS = 1e-5
HEAD = 64
LANES = 128
M_HEADS = 4
DF_HEADS = 4
SA_HEADS = 4
IDX_HEADS = 8
INDEX_TOPK_MAX = 256
CONV_WIDTH = 4
N_EXPERTS = 32
TOP_K = 4
SWIGLU_LIMIT = 7.0
SWIGLU_ALPHA = 1.702
NEG = -1e30
INT_MIN = -(2 ** 31)
MIB = 1024 * 1024

_O_MQ, _O_MK, _O_MV, _O_MI, _O_MF, _O_MO = 0, 256, 512, 768, 772, 776
_O_DQ, _O_DK, _O_DV = 1032, 1544, 2056
_O_SQ, _O_SK, _O_SV = 2568, 2824, 3080
_O_XQ, _O_XK, _O_XW, _N_IN = 3336, 3848, 3912, 3920
_C_MQK, _C_MV, _C_MO, _C_DQ, _C_DK, _C_DV = 0, 512, 768, 1024, 1536, 2048
_C_SQ, _C_SK, _C_SV, _C_XQ, _C_XK, _C_MISC, _N_COLS = 2560, 2816, 3072, 3328, 3840, 3968, 4096
_L_XW, _L_MI, _L_MF = 0, 8, 12


def _cparams(sem, vmem_mib):
    return pltpu.CompilerParams(dimension_semantics=sem, vmem_limit_bytes=vmem_mib * MIB)


def _nt_dot(a, b):
    return lax.dot_general(a, b, (((1,), (1,)), ((), ())), preferred_element_type=F32)


def _residual_ln(x, y, gate, g, b, alpha):
    z = alpha * x + (1.0 + gate) * y
    mu = jnp.mean(z, axis=-1, keepdims=True)
    d = z - mu
    var = jnp.mean(d * d, axis=-1, keepdims=True)
    return d * lax.rsqrt(var + LN_EPS) * g + b


def _prefix_rows(a, op, ident):
    n = a.shape[0]
    row = lax.broadcasted_iota(I32, a.shape, 0)
    k = 1
    while k < n:
        a = op(a, jnp.where(row >= k, pltpu.roll(a, k, 0), ident))
        k *= 2
    return a


def _prefix_lanes(a, op, ident):
    n = a.shape[1]
    col = lax.broadcasted_iota(I32, a.shape, 1)
    k = 1
    while k < n:
        a = op(a, jnp.where(col >= k, pltpu.roll(a, k, 1), ident))
        k *= 2
    return a


def _mod_kernel(c_ref, w_ref, b_ref, o_ref):
    c = c_ref[...]
    ca = c * jax.nn.sigmoid(c)
    o_ref[...] = jnp.dot(ca, w_ref[...], precision=lax.Precision.HIGHEST,
                         preferred_element_type=F32) + b_ref[...]


def _modulation(c, ada_w, ada_b):
    depth, d, nm = ada_w.shape
    b = c.shape[0]
    return pl.pallas_call(
        _mod_kernel,
        grid=(depth, nm // d),
        in_specs=[pl.BlockSpec((b, d), lambda l, j: (0, 0)),
                  pl.BlockSpec((None, d, d), lambda l, j: (l, 0, j)),
                  pl.BlockSpec((None, 1, d), lambda l, j: (l, 0, j))],
        out_specs=pl.BlockSpec((None, b, d), lambda l, j: (l, 0, j)),
        out_shape=jax.ShapeDtypeStruct((depth, b, nm), F32),
        compiler_params=_cparams(("parallel", "parallel"), 32),
        name="adaln_modulation",
    )(c, ada_w, ada_b.reshape(depth, 1, nm))


def _inproj_kernel(has_res, alpha, *refs):
    if has_res:
        (x_ref, y_ref, gp_ref, lg_ref, lb_ref, sc_ref, sh_ref, w_ref, c_ref, s1_ref, s2_ref,
         ng_ref, nb_ref, ms_ref, mb_ref,
         xo_ref, mqk_ref, mv_ref, mo_ref, dq_ref, dk_ref, dv_ref, sq_ref, sk_ref, sv_ref,
         xq_ref, xk_ref, misc_ref) = refs
        x = _residual_ln(x_ref[...], y_ref[...], gp_ref[...], lg_ref[...], lb_ref[...], alpha)
        xo_ref[...] = x
    else:
        (x_ref, sc_ref, sh_ref, w_ref, c_ref, s1_ref, s2_ref, ng_ref, nb_ref, ms_ref, mb_ref,
         mqk_ref, mv_ref, mo_ref, dq_ref, dk_ref, dv_ref, sq_ref, sk_ref, sv_ref,
         xq_ref, xk_ref, misc_ref) = refs
        x = x_ref[...]
    u = (x * (1.0 + sc_ref[...]) + sh_ref[...]).astype(BF16)
    cosv, sin_lo, sin_hi = c_ref[...], s1_ref[...], s2_ref[...]

    def mm(c0, width):
        return jnp.dot(u, w_ref[:, c0:c0 + width], preferred_element_type=F32)

    def rope(xc):
        return (xc * cosv + pltpu.roll(xc, LANES - HEAD // 2, 1) * sin_lo
                + pltpu.roll(xc, HEAD // 2, 1) * sin_hi)

    def rope_store(out_ref, c0, width):
        r = mm(c0, width)
        for c in range(width // LANES):
            out_ref[:, c * LANES:(c + 1) * LANES] = rope(r[:, c * LANES:(c + 1) * LANES]).astype(out_ref.dtype)

    mqk_ref[...] = mm(_C_MQK, 512)
    mv_ref[...] = mm(_C_MV, 256).astype(BF16)
    mo_ref[...] = mm(_C_MO, 256).astype(BF16)
    rope_store(dq_ref, _C_DQ, 512)
    rope_store(dk_ref, _C_DK, 512)
    dv_ref[...] = mm(_C_DV, 512).astype(BF16)
    rope_store(sq_ref, _C_SQ, 256)
    rope_store(sk_ref, _C_SK, 256)
    sv_ref[...] = mm(_C_SV, 256).astype(BF16)
    rope_store(xq_ref, _C_XQ, 512)

    r = mm(_C_XK, LANES)
    lane = lax.broadcasted_iota(I32, r.shape, 1)
    first = lane < HEAD
    mu = jnp.sum(jnp.where(first, r, 0.0), axis=1, keepdims=True) * (1.0 / HEAD)
    d = r - mu
    var = jnp.sum(jnp.where(first, d * d, 0.0), axis=1, keepdims=True) * (1.0 / HEAD)
    xn = d * lax.rsqrt(var + LN_EPS) * ng_ref[...] + nb_ref[...]
    xk_ref[...] = rope(xn).astype(BF16)

    misc_ref[...] = mm(_C_MISC, LANES) * ms_ref[...] + mb_ref[...]


def _inproj(x, res, sc, sh, w, tabs, ng2, nb2, ms, mb, alpha, tm):
    b, s, d = x.shape
    grid = (b, s // tm)
    row = lambda width: pl.BlockSpec((None, tm, width), lambda bi, i: (bi, i, 0))
    per_b = pl.BlockSpec((None, 1, d), lambda bi, i: (bi, 0, 0))
    vec = lambda width: pl.BlockSpec((1, width), lambda bi, i: (0, 0))
    in_specs, args = [row(d)], [x]
    if res is not None:
        y, gprev, lg, lb = res
        in_specs += [row(d), per_b, vec(d), vec(d)]
        args += [y, gprev, lg, lb]
    in_specs += [per_b, per_b, pl.BlockSpec((d, _N_COLS), lambda bi, i: (0, 0)),
                 row(LANES), row(LANES), row(LANES), vec(LANES), vec(LANES), vec(LANES), vec(LANES)]
    args += [sc, sh, w, *tabs, ng2, nb2, ms, mb]
    outs = [("mqk", 512, F32), ("mv", 256, BF16), ("mo", 256, BF16),
            ("dq", 512, BF16), ("dk", 512, BF16), ("dv", 512, BF16),
            ("sq", 256, BF16), ("sk", 256, BF16), ("sv", 256, BF16),
            ("xq", 512, BF16), ("xk", LANES, BF16), ("misc", LANES, F32)]
    if res is not None:
        outs = [("x", d, F32)] + outs
    return pl.pallas_call(
        functools.partial(_inproj_kernel, res is not None, alpha),
        grid=grid,
        in_specs=in_specs,
        out_specs=[row(wd) for _, wd, _ in outs],
        out_shape=[jax.ShapeDtypeStruct((b, s, wd), dt) for _, wd, dt in outs],
        compiler_params=_cparams(("parallel", "parallel"), 48),
        name="inproj",
    )(*args)


def _mlstm_kernel(mqk_ref, mv_ref, mo_ref, misc_ref, cw_ref, cb_ref, g_ref, out_ref,
                  q_scr, k_scr, at_scr, mx_scr, nrm_scr, *, seq, qb):
    nq = seq // qb
    cw = cw_ref[...]
    cb = cb_ref[...]

    def conv_body(c, carry):
        t0 = pl.multiple_of(c * qb, qb)
        main = mqk_ref[pl.ds(t0, qb), :]
        prev = mqk_ref[pl.ds(pl.multiple_of(jnp.maximum(t0 - 8, 0), 8), 8), :]
        prev = jnp.where(c > 0, prev, 0.0)
        ext = jnp.concatenate([prev, main], axis=0)
        y = cb + cw[CONV_WIDTH - 1:CONV_WIDTH] * main
        for j in range(CONV_WIDTH - 1):
            y = y + cw[j:j + 1] * pltpu.roll(ext, CONV_WIDTH - 1 - j, 0)[8:]
        y = y * jax.nn.sigmoid(y)
        q_scr[pl.ds(t0, qb), :] = (y[:, :256] * (HEAD ** -0.5)).astype(BF16)
        k_scr[pl.ds(t0, qb), :] = y[:, 256:].astype(BF16)
        return carry

    lax.fori_loop(0, nq, conv_body, 0)

    misc = misc_ref[...]
    lf = jnp.minimum(misc, 0.0) - jnp.log(1.0 + jnp.exp(-jnp.abs(misc)))
    bc = _prefix_rows(lf, jnp.add, 0.0)
    a = pltpu.roll(misc, _L_MF - _L_MI, 1) - bc
    mx = _prefix_rows(a, jnp.maximum, -jnp.inf)
    mx_scr[...] = mx
    nrm_scr[...] = jnp.exp(-(bc + mx))
    at_scr[...] = a.T

    lane = lax.broadcasted_iota(I32, (qb, LANES), 1)
    for p in range(M_HEADS // 2):
        cs = slice(p * LANES, (p + 1) * LANES)

        def qblock(i, carry, p=p, cs=cs):
            t0 = pl.multiple_of(i * qb, qb)
            q2 = q_scr[pl.ds(t0, qb), cs]
            rowi = t0 + lax.broadcasted_iota(I32, (qb, qb), 0)
            res = []
            for hh in range(2):
                h = 2 * p + hh
                hmask = (lane >= HEAD) if hh else (lane < HEAD)
                qm = jnp.where(hmask, q2, jnp.zeros_like(q2))
                mcol = mx_scr[pl.ds(t0, qb), _L_MF + h:_L_MF + h + 1]

                def kv(j, acc, h=h, qm=qm, mcol=mcol):
                    num, den = acc
                    s0 = pl.multiple_of(j * qb, qb)
                    k2 = k_scr[pl.ds(s0, qb), cs]
                    v2 = mv_ref[pl.ds(s0, qb), cs]
                    s = _nt_dot(qm, k2)
                    arow = at_scr[_L_MF + h:_L_MF + h + 1, pl.ds(s0, qb)]
                    coli = s0 + lax.broadcasted_iota(I32, (qb, qb), 1)
                    e = jnp.where(coli <= rowi, jnp.exp(arow - mcol), 0.0)
                    pm = s * e
                    den = den + jnp.sum(pm, axis=1, keepdims=True)
                    num = num + jnp.dot(pm.astype(BF16), v2, preferred_element_type=F32)
                    return num, den

                num, den = lax.fori_loop(0, i + 1, kv, (jnp.zeros((qb, LANES), F32), jnp.zeros((qb, 1), F32)))
                nrm = nrm_scr[pl.ds(t0, qb), _L_MF + h:_L_MF + h + 1]
                hv = num / jnp.maximum(jnp.abs(den), nrm)
                mu = jnp.sum(jnp.where(hmask, hv, 0.0), axis=1, keepdims=True) * (1.0 / HEAD)
                d = jnp.where(hmask, hv - mu, 0.0)
                var = jnp.sum(d * d, axis=1, keepdims=True) * (1.0 / HEAD)
                res.append(d * lax.rsqrt(var + LN_EPS))
            gate = jax.nn.sigmoid(mo_ref[pl.ds(t0, qb), cs].astype(F32))
            out_ref[pl.ds(t0, qb), cs] = ((res[0] + res[1]) * g_ref[:, cs] * gate).astype(BF16)
            return carry

        lax.fori_loop(0, nq, qblock, 0)


def _mlstm(mqk, mv, mo, misc, conv_w, conv_b, norm_g, qb):
    b, s, _ = mqk.shape
    full = lambda width: pl.BlockSpec((None, s, width), lambda bi: (bi, 0, 0))
    vec = lambda rows, width: pl.BlockSpec((rows, width), lambda bi: (0, 0))
    return pl.pallas_call(
        functools.partial(_mlstm_kernel, seq=s, qb=qb),
        grid=(b,),
        in_specs=[full(512), full(256), full(256), full(LANES), vec(CONV_WIDTH, 512), vec(1, 512), vec(1, 256)],
        out_specs=full(256),
        out_shape=jax.ShapeDtypeStruct((b, s, 256), BF16),
        scratch_shapes=[pltpu.VMEM((s, 256), BF16), pltpu.VMEM((s, 256), BF16),
                        pltpu.VMEM((LANES, s), F32), pltpu.VMEM((s, LANES), F32), pltpu.VMEM((s, LANES), F32)],
        compiler_params=_cparams(("parallel",), 48),
        name="mlstm",
    )(mqk, mv, mo, misc, conv_w, conv_b, norm_g)


def _softmax_sweep(qm, k_ref, v_ref, cs, nchunk, kc, logit_bias):
    qrows = qm.shape[0]

    def kv(j, carry):
        m, l, acc = carry
        s0 = pl.multiple_of(j * kc, kc)
        s = _nt_dot(qm, k_ref[pl.ds(s0, kc), cs]) + logit_bias(s0)
        m_new = jnp.maximum(m, jnp.max(s, axis=1, keepdims=True))
        alpha = jnp.exp(m - m_new)
        p = jnp.exp(s - m_new)
        l = alpha * l + jnp.sum(p, axis=1, keepdims=True)
        acc = alpha * acc + jnp.dot(p.astype(BF16), v_ref[pl.ds(s0, kc), cs], preferred_element_type=F32)
        return m_new, l, acc

    init = (jnp.full((qrows, 1), NEG, F32), jnp.zeros((qrows, 1), F32), jnp.zeros((qrows, LANES), F32))
    _, l, acc = lax.fori_loop(0, nchunk, kv, init)
    return acc / l


def _diff_kernel(q_ref, k_ref, v_ref, lam_ref, g_ref, out_ref, *, qb):
    i = pl.program_id(2)
    t0 = i * qb
    lane = lax.broadcasted_iota(I32, (qb, LANES), 1)
    rowi = t0 + lax.broadcasted_iota(I32, (qb, qb), 0)
    coll = lax.broadcasted_iota(I32, (qb, qb), 1)
    q2 = q_ref[...]
    full = slice(0, LANES)

    def causal(s0):
        return jnp.where(s0 + coll <= rowi, 0.0, NEG)

    maps = []
    for hh in range(2):
        hmask = (lane >= HEAD) if hh else (lane < HEAD)
        qm = jnp.where(hmask, q2, jnp.zeros_like(q2))
        maps.append(_softmax_sweep(qm, k_ref, v_ref, full, i + 1, qb, causal))
    o = maps[0] - lam_ref[...] * maps[1]
    o = o * lax.rsqrt(jnp.mean(o * o, axis=1, keepdims=True) + LN_EPS) * g_ref[...]
    out_ref[...] = o.astype(BF16)


def _diff_attention(dq, dk, dv, lam_vec, g_eff, qb):
    b, s, _ = dq.shape
    return pl.pallas_call(
        functools.partial(_diff_kernel, qb=qb),
        grid=(b, DF_HEADS, s // qb),
        in_specs=[pl.BlockSpec((None, qb, LANES), lambda bi, h, i: (bi, i, h)),
                  pl.BlockSpec((None, s, LANES), lambda bi, h, i: (bi, 0, h)),
                  pl.BlockSpec((None, s, LANES), lambda bi, h, i: (bi, 0, h)),
                  pl.BlockSpec((1, LANES), lambda bi, h, i: (0, 0)),
                  pl.BlockSpec((1, LANES), lambda bi, h, i: (0, h))],
        out_specs=pl.BlockSpec((None, qb, LANES), lambda bi, h, i: (bi, i, h)),
        out_shape=jax.ShapeDtypeStruct((b, s, DF_HEADS * LANES), BF16),
        compiler_params=_cparams(("parallel", "parallel", "parallel"), 32),
        name="diff_attention",
    )(dq, dk, dv, lam_vec, g_eff)


def _dsa_kernel(xq_ref, misc_ref, xk_ref, sq_ref, sk_ref, sv_ref, out_ref,
                key_scr, bias_scr, thr_scr, cut_scr, *, qb, topk, idx_bits):
    i = pl.program_id(1)
    t0 = i * qb
    nchunk = i + 1
    lane = lax.broadcasted_iota(I32, (qb, LANES), 1)
    lo = lane < HEAD
    rowi = t0 + lax.broadcasted_iota(I32, (qb, qb), 0)
    coll = lax.broadcasted_iota(I32, (qb, qb), 1)
    nsub = qb // LANES

    xq = xq_ref[...]
    misc = misc_ref[...]
    qs, ws = [], []
    for j in range(IDX_HEADS):
        slab = xq[:, (j // 2) * LANES:(j // 2 + 1) * LANES]
        qs.append(jnp.where(lo if j % 2 == 0 else jnp.logical_not(lo), slab, jnp.zeros_like(slab)))
        ws.append(misc[:, _L_XW + j:_L_XW + j + 1])

    def score_chunk(c, carry):
        s0 = pl.multiple_of(c * qb, qb)
        xk = xk_ref[pl.ds(s0, qb), :]
        acc = jnp.zeros((qb, qb), F32)
        for j in range(IDX_HEADS):
            acc = acc + ws[j] * jnp.maximum(_nt_dot(qs[j], xk), 0.0)
        bits = lax.bitcast_convert_type(acc, I32)
        key = jnp.where(bits < 0, bits ^ jnp.int32(0x7FFFFFFF), bits)
        key_scr[:, pl.ds(s0, qb)] = jnp.where(s0 + coll <= rowi, key, INT_MIN)
        return carry

    lax.fori_loop(0, nchunk, score_chunk, 0)

    def count(pred):
        def body(c, cnt):
            s0 = pl.multiple_of(c * qb, qb)
            blk = key_scr[:, pl.ds(s0, qb)]
            for sb in range(nsub):
                cnt = cnt + jnp.where(pred(blk[:, sb * LANES:(sb + 1) * LANES], s0 + sb * LANES), 1, 0)
            return cnt
        cnt = lax.fori_loop(0, nchunk, body, jnp.zeros((qb, LANES), I32))
        return jnp.sum(cnt, axis=1, keepdims=True)

    thr_scr[...] = jnp.full((qb, 1), INT_MIN, I32)
    cut_scr[...] = jnp.zeros((qb, 1), I32)

    @pl.when((i + 1) * qb > topk)
    def _select():
        def bit_body(it, carry):
            thr, cnt_thr = carry
            cand = thr + jnp.left_shift(jnp.int32(1), 31 - it)
            candb = jnp.broadcast_to(cand, (qb, LANES))
            tot = count(lambda blk, s0: blk >= candb)
            ok = tot >= topk
            return jnp.where(ok, cand, thr), jnp.where(ok, tot, cnt_thr)

        thr, cnt_thr = lax.fori_loop(
            0, 32, bit_body, (jnp.full((qb, 1), INT_MIN, I32), jnp.full((qb, 1), 2 ** 30, I32)))
        live = thr > INT_MIN
        thr_scr[...] = thr
        cut_scr[...] = jnp.where(live, 2 ** 30, 0)
        tied = jnp.max(jnp.where(live & (cnt_thr > topk), 1, 0))

        @pl.when(tied > 0)
        def _ties():
            thrb = jnp.broadcast_to(thr, (qb, LANES))
            need = topk - count(lambda blk, s0: blk > thrb)

            def cut_body(it, cut):
                cand = cut + jnp.left_shift(jnp.int32(1), idx_bits - 1 - it)
                candb = jnp.broadcast_to(cand, (qb, LANES))
                tot = count(lambda blk, s0: (blk == thrb) & (s0 + lane < candb))
                return jnp.where(tot <= need, cand, cut)

            cut = lax.fori_loop(0, idx_bits, cut_body, jnp.zeros((qb, 1), I32))
            cut_scr[...] = jnp.where(live, cut, 0)

    thr = thr_scr[...]
    cut = cut_scr[...]

    def bias_chunk(c, carry):
        s0 = pl.multiple_of(c * qb, qb)
        blk = key_scr[:, pl.ds(s0, qb)]
        sel = (blk > thr) | ((blk == thr) & (s0 + coll < cut))
        bias_scr[:, pl.ds(s0, qb)] = jnp.where(sel, 0.0, NEG)
        return carry

    lax.fori_loop(0, nchunk, bias_chunk, 0)

    def bias(s0):
        return bias_scr[:, pl.ds(s0, qb)]

    for p in range(SA_HEADS // 2):
        cs = slice(p * LANES, (p + 1) * LANES)
        q2 = sq_ref[:, cs]
        halves = []
        for hh in range(2):
            qm = jnp.where(jnp.logical_not(lo) if hh else lo, q2, jnp.zeros_like(q2))
            halves.append(_softmax_sweep(qm, sk_ref, sv_ref, cs, nchunk, qb, bias))
        out_ref[:, cs] = jnp.where(lo, halves[0], halves[1]).astype(BF16)


def _dsa_attention(xq, misc, xk, sq, sk, sv, qb, topk):
    b, s, _ = sq.shape
    blk = lambda width: pl.BlockSpec((None, qb, width), lambda bi, i: (bi, i, 0))
    full = lambda width: pl.BlockSpec((None, s, width), lambda bi, i: (bi, 0, 0))
    return pl.pallas_call(
        functools.partial(_dsa_kernel, qb=qb, topk=topk, idx_bits=int(s).bit_length() + 1),
        grid=(b, s // qb),
        in_specs=[blk(512), blk(LANES), full(LANES), blk(256), full(256), full(256)],
        out_specs=blk(256),
        out_shape=jax.ShapeDtypeStruct((b, s, 256), BF16),
        scratch_shapes=[pltpu.VMEM((qb, s), I32), pltpu.VMEM((qb, s), F32),
                        pltpu.VMEM((qb, 1), I32), pltpu.VMEM((qb, 1), I32)],
        compiler_params=_cparams(("parallel", "parallel"), 32),
        name="dsa_attention",
    )(xq, misc, xk, sq, sk, sv)


def _outproj_kernel(hm_ref, hd_ref, hs_ref, x_ref, w_ref, g1_ref, lg_ref, lb_ref, sc_ref, sh_ref,
                    rw_ref, rb_ref, x1_ref, u2_ref, comb_ref, combt_ref, *, alpha):
    y = (jnp.dot(hm_ref[...], w_ref[0:256, :], preferred_element_type=F32)
         + jnp.dot(hd_ref[...], w_ref[256:768, :], preferred_element_type=F32)
         + jnp.dot(hs_ref[...], w_ref[768:1024, :], preferred_element_type=F32))
    x1 = _residual_ln(x_ref[...], y, g1_ref[...], lg_ref[...], lb_ref[...], alpha)
    x1_ref[...] = x1
    u2 = x1 * (1.0 + sc_ref[...]) + sh_ref[...]
    u2_ref[...] = u2.astype(BF16)
    logits = jnp.dot(u2, rw_ref[...], precision=lax.Precision.HIGHEST,
                     preferred_element_type=F32) + rb_ref[...]
    lane = lax.broadcasted_iota(I32, logits.shape, 1)
    work = logits
    sel = lane < 0
    top = None
    for k in range(TOP_K):
        mx = jnp.max(work, axis=1, keepdims=True)
        if k == 0:
            top = mx
        first = jnp.min(jnp.where(work == mx, lane, LANES), axis=1, keepdims=True)
        hit = lane == first
        sel = sel | hit
        work = jnp.where(hit, -jnp.inf, work)
    e = jnp.where(sel, jnp.exp(logits - top), 0.0)
    comb = e / jnp.sum(e, axis=1, keepdims=True)
    comb_ref[...] = comb
    combt_ref[...] = comb.T


def _outproj(hm, hd, hs, x, w_out, g1, lg, lb, sc2, sh2, rw, rb, alpha, tm):
    b, s, d = x.shape
    row = lambda width: pl.BlockSpec((None, tm, width), lambda bi, i: (bi, i, 0))
    per_b = pl.BlockSpec((None, 1, d), lambda bi, i: (bi, 0, 0))
    vec = lambda width: pl.BlockSpec((1, width), lambda bi, i: (0, 0))
    return pl.pallas_call(
        functools.partial(_outproj_kernel, alpha=alpha),
        grid=(b, s // tm),
        in_specs=[row(256), row(512), row(256), row(d), pl.BlockSpec((d, d), lambda bi, i: (0, 0)),
                  per_b, vec(d), vec(d), per_b, per_b,
                  pl.BlockSpec((d, LANES), lambda bi, i: (0, 0)), vec(LANES)],
        out_specs=[row(d), row(d), row(LANES), pl.BlockSpec((None, LANES, tm), lambda bi, i: (bi, 0, i))],
        out_shape=[jax.ShapeDtypeStruct((b, s, d), F32), jax.ShapeDtypeStruct((b, s, d), BF16),
                   jax.ShapeDtypeStruct((b, s, LANES), F32), jax.ShapeDtypeStruct((b, LANES, s), F32)],
        compiler_params=_cparams(("parallel", "parallel"), 40),
        name="outproj_router",
    )(hm, hd, hs, x, w_out, g1, lg, lb, sc2, sh2, rw, rb)


def _moe_kernel(u_ref, comb_ref, combt_ref, w1_ref, b1_ref, w2_ref, b2_ref, y_ref,
                rankc_scr, rankr_scr, *, tb, rt, sub, dff):
    e = pl.program_id(1)

    @pl.when(e == 0)
    def _init():
        y_ref[...] = jnp.zeros_like(y_ref)
        mc = jnp.where(comb_ref[...] != 0.0, 1.0, 0.0)
        rankc_scr[...] = _prefix_rows(mc, jnp.add, 0.0) - mc
        mr = jnp.where(combt_ref[0:N_EXPERTS, :] != 0.0, 1.0, 0.0)
        rankr_scr[...] = _prefix_lanes(mr, jnp.add, 0.0) - mr

    ce_row = combt_ref[pl.ds(e, 1), :]
    rank_row = rankr_scr[pl.ds(e, 1), :]
    lane = lax.broadcasted_iota(I32, (tb, LANES), 1)
    pick = lane == e
    ce_col = jnp.sum(jnp.where(pick, comb_ref[...], 0.0), axis=1, keepdims=True)
    rank_col = jnp.sum(jnp.where(pick, rankc_scr[...], 0.0), axis=1, keepdims=True)
    routed_row = ce_row != 0.0
    n_routed = jnp.sum(jnp.where(routed_row, 1, 0))
    ntile = (n_routed + rt - 1) // rt

    def tile(r, carry):
        r0 = r * rt
        rid = (r0 + lax.broadcasted_iota(I32, (rt, tb), 0)).astype(F32)
        gat = jnp.where((rank_row == rid) & routed_row, 1.0, 0.0).astype(BF16)
        xg = jnp.dot(gat, u_ref[...], preferred_element_type=F32).astype(BF16)
        hh = jnp.dot(xg, w1_ref[...], preferred_element_type=F32) + b1_ref[...]
        gate = jnp.minimum(hh[:, :dff], SWIGLU_LIMIT)
        up = jnp.clip(hh[:, dff:], -SWIGLU_LIMIT, SWIGLU_LIMIT)
        act = ((up + 1.0) * (gate * jax.nn.sigmoid(SWIGLU_ALPHA * gate))).astype(BF16)
        yy = (jnp.dot(act, w2_ref[...], preferred_element_type=F32) + b2_ref[...]).astype(BF16)
        cid = (r0 + lax.broadcasted_iota(I32, (sub, rt), 1)).astype(F32)
        for sb in range(tb // sub):
            rs = slice(sb * sub, (sb + 1) * sub)
            cc = ce_col[rs]
            sca = jnp.where((rank_col[rs] == cid) & (cc != 0.0), 1.0, 0.0).astype(BF16)
            y_ref[rs, :] += cc * jnp.dot(sca, yy, preferred_element_type=F32)
        return carry

    lax.fori_loop(0, ntile, tile, 0)


def _moe(u2, comb, combt, w1, b1, w2, b2, rt):
    b, s, d = u2.shape
    n_e, _, dff2 = w1.shape
    tb = s
    sub = min(512, tb)
    return pl.pallas_call(
        functools.partial(_moe_kernel, tb=tb, rt=rt, sub=sub, dff=dff2 // 2),
        grid=(b, n_e),
        in_specs=[pl.BlockSpec((None, tb, d), lambda t, e: (t, 0, 0)),
                  pl.BlockSpec((None, tb, LANES), lambda t, e: (t, 0, 0)),
                  pl.BlockSpec((None, LANES, tb), lambda t, e: (t, 0, 0)),
                  pl.BlockSpec((None, d, dff2), lambda t, e: (e, 0, 0)),
                  pl.BlockSpec((None, 1, dff2), lambda t, e: (e, 0, 0)),
                  pl.BlockSpec((None, dff2 // 2, d), lambda t, e: (e, 0, 0)),
                  pl.BlockSpec((None, 1, d), lambda t, e: (e, 0, 0))],
        out_specs=pl.BlockSpec((None, tb, d), lambda t, e: (t, 0, 0)),
        out_shape=jax.ShapeDtypeStruct((b, s, d), F32),
        scratch_shapes=[pltpu.VMEM((tb, LANES), F32), pltpu.VMEM((N_EXPERTS, tb), F32)],
        compiler_params=_cparams(("parallel", "arbitrary"), 56),
        name="moe_experts",
    )(u2, comb, combt, w1, b1, w2, b2)


def _final_kernel(x_ref, y_ref, g_ref, lg_ref, lb_ref, o_ref, *, alpha):
    o_ref[...] = _residual_ln(x_ref[...], y_ref[...], g_ref[...], lg_ref[...], lb_ref[...], alpha)


def _final_norm(x, y, gate, lg, lb, alpha, tm):
    b, s, d = x.shape
    row = pl.BlockSpec((None, tm, d), lambda bi, i: (bi, i, 0))
    per_b = pl.BlockSpec((None, 1, d), lambda bi, i: (bi, 0, 0))
    vec = pl.BlockSpec((1, d), lambda bi, i: (0, 0))
    return pl.pallas_call(
        functools.partial(_final_kernel, alpha=alpha),
        grid=(b, s // tm),
        in_specs=[row, row, per_b, vec, vec],
        out_specs=row,
        out_shape=jax.ShapeDtypeStruct((b, s, d), F32),
        compiler_params=_cparams(("parallel", "parallel"), 32),
        name="final_norm",
    )(x, y, gate, lg, lb)


def _pack_w_in(w_in):
    depth, d, _ = w_in.shape
    cols = lambda a, n: w_in[:, :, a:a + n]
    xk = cols(_O_XK, 64)
    pad = jnp.zeros((depth, d, LANES - IDX_HEADS - 2 * M_HEADS), w_in.dtype)
    qscale = HEAD ** -0.5
    return jnp.concatenate(
        [cols(_O_MQ, 512), cols(_O_MV, 256), cols(_O_MO, 256),
         cols(_O_DQ, 512) * qscale, cols(_O_DK, 512), cols(_O_DV, 512),
         cols(_O_SQ, 256) * qscale, cols(_O_SK, 256), cols(_O_SV, 256),
         cols(_O_XQ, 512) * qscale, xk, xk,
         cols(_O_XW, IDX_HEADS), cols(_O_MI, 2 * M_HEADS), pad], axis=-1).astype(BF16)


def kernel(x, c, positions, w_in, m_gate_bias, m_conv_w, m_conv_b, m_norm_g, df_lambda, df_norm_g, idx_norm_g, idx_norm_b, w_out, ada_w, ada_b, ln1_g, ln1_b, ln2_g, ln2_b, router_w, router_b, w1, b1, w2, b2):
    b, s, d = x.shape
    depth = w_in.shape[0]
    alpha = (2 * 4) ** 0.25
    qb = 256
    tm = min(512, s)
    topk = min(INDEX_TOPK_MAX, s // 4)
    rt = 320

    inv = ROPE_THETA ** (-jnp.arange(0, HEAD, 2, dtype=F32) / HEAD)
    ang = positions.astype(F32)[..., None] * inv
    cos, sin = jnp.cos(ang), jnp.sin(ang)
    zero = jnp.zeros_like(sin)
    tabs = (jnp.concatenate([cos, cos, cos, cos], -1),
            jnp.concatenate([-sin, zero, -sin, zero], -1),
            jnp.concatenate([zero, sin, zero, sin], -1))

    mod = _modulation(c, ada_w, ada_b)
    w_in_p = _pack_w_in(w_in)
    w_out_b = w_out.astype(BF16)
    w1_b = w1.astype(BF16)
    w2_b = w2.astype(BF16)
    rw = jnp.pad(router_w, ((0, 0), (0, 0), (0, LANES - N_EXPERTS)))
    rb = jnp.pad(router_b, ((0, 0), (0, LANES - N_EXPERTS)), constant_values=NEG)
    misc_tail = LANES - IDX_HEADS - 2 * M_HEADS
    ms = jnp.concatenate([jnp.full((IDX_HEADS,), IDX_HEADS ** -0.5, F32), jnp.ones((2 * M_HEADS,), F32),
                          jnp.zeros((misc_tail,), F32)])[None, :]
    lf = df_lambda.astype(F32)
    lam_raw = jnp.exp(jnp.sum(lf[:, 0] * lf[:, 1], -1)) - jnp.exp(jnp.sum(lf[:, 2] * lf[:, 3], -1))

    res = None
    for l in range(depth):
        m6 = mod[l].reshape(b, 6, 1, d)
        sh1, sc1, g1, sh2, sc2, g2 = [m6[:, k] for k in range(6)]
        lam_init = 0.8 - 0.6 * math.exp(-0.3 * l)
        mb = jnp.concatenate([jnp.zeros((IDX_HEADS,), F32), m_gate_bias[l], jnp.zeros((misc_tail,), F32)])[None, :]
        ng2 = jnp.concatenate([idx_norm_g[l], idx_norm_g[l]])[None, :]
        nb2 = jnp.concatenate([idx_norm_b[l], idx_norm_b[l]])[None, :]
        outs = _inproj(x, res, sc1, sh1, w_in_p[l], tabs, ng2, nb2, ms, mb, alpha, tm)
        if res is not None:
            x, outs = outs[0], outs[1:]
        mqk, mv, mo, dq, dk, dv, sq, sk, sv, xq, xk, misc = outs
        hm = _mlstm(mqk, mv, mo, misc, m_conv_w[l], m_conv_b[l][None, :], m_norm_g[l][None, :], qb)
        lam_vec = jnp.full((1, LANES), lam_raw[l] + lam_init, F32)
        hd = _diff_attention(dq, dk, dv, lam_vec, (df_norm_g[l] * (1.0 - lam_init))[None, :], qb)
        hs = _dsa_attention(xq, misc, xk, sq, sk, sv, qb, topk)
        x1, u2, comb, combt = _outproj(hm, hd, hs, x, w_out_b[l], g1, ln1_g[l][None, :], ln1_b[l][None, :],
                                       sc2, sh2, rw[l], rb[l][None, :], alpha, tm)
        y = _moe(u2, comb, combt, w1_b[l], b1[l][:, None, :], w2_b[l], b2[l][:, None, :], rt)
        x = x1
        res = (y, g2, ln2_g[l][None, :], ln2_b[l][None, :])
    return _final_norm(x, res[0], res[1], res[2], res[3], alpha, tm)
```
